```python
import jax, jax.numpy as jnp
from jax import lax
import numpy as np

D_MODEL = 2048
BATCH = 1
SEQ = 8192
DEPTH = 2
DEC_BATCH = 32
DEC_SEQ = 4
PAST_LEN = 8192
PAGE_SIZE = 128

N_META = 16
HD_A = 64
H_A = (D_MODEL // 2) // HD_A
D_A = H_A * HD_A
LORA_W = 64
LORA_A = 64
SHIFT_DIM = 3 * D_A + LORA_W + LORA_A
HD_B = 128
H_B = (D_MODEL // 2) // HD_B
D_B = H_B * HD_B
N_IN0 = SHIFT_DIM + D_A + 3 * D_B + H_B + D_B
H_C = 8
DK_C = (D_MODEL // 2) // H_C
DV_C = 2 * DK_C
DQK_C = H_C * DK_C
DVW_C = H_C * DV_C
N_IN1 = 2 * DQK_C + 2 * DVW_C
Q_BLOCK = 128
RET_CHUNK = 128
RMS_EPS = 1e-6
LNX_EPS = 64e-5
ROPE_BASE = 10000.0

kernel_name = 'hybrid_rwkv7_fox_retention_step'


def rms_norm(x, g):
    xf = x.astype(jnp.float32)
    y = xf * lax.rsqrt(jnp.mean(xf * xf, axis=-1, keepdims=True) + RMS_EPS)
    return (y * g.astype(jnp.float32)).astype(x.dtype)


def rwkv7_mix(cols, prev_row, s0, mu, w0, w2, a0, a2, k_k, k_a, r_k, lnx_g, lnx_b):
    f32 = jnp.float32
    B, T, _ = cols.shape
    c = cols.astype(f32)
    prev = jnp.concatenate([prev_row.astype(f32)[:, None, :], c[:, :-1]], axis=1)
    xs = c + (prev - c) * mu.astype(f32)
    r, k, v, wlo, alo = jnp.split(xs, [D_A, 2 * D_A, 3 * D_A, 3 * D_A + LORA_W], axis=-1)
    w_log = -jax.nn.softplus(-(w0.astype(f32) + jnp.tanh(wlo) @ w2.astype(f32))) - 0.5
    decay = jnp.exp(-jnp.exp(w_log))
    a = jax.nn.sigmoid(a0.astype(f32) + alo @ a2.astype(f32))
    heads = lambda t: t.reshape(B, T, H_A, HD_A)
    kk = heads(k * k_k.astype(f32))
    kk = kk * lax.rsqrt(jnp.sum(kk * kk, axis=-1, keepdims=True) + 1e-12)
    k = k * (1.0 + (a - 1.0) * k_a.astype(f32))
    r, k, v, a, decay = heads(r), heads(k), heads(v), heads(a), heads(decay)
    a_vec = -kk
    b_vec = kk * a

    def step(S, inp):
        r_t, w_t, k_t, v_t, a_t, b_t = inp
        sa = jnp.einsum('bhij,bhj->bhi', S, a_t)
        S = S * w_t[:, :, None, :] + sa[..., None] * b_t[:, :, None, :] + v_t[..., None] * k_t[:, :, None, :]
        return S, jnp.einsum('bhij,bhj->bhi', S, r_t)

    seq = tuple(jnp.swapaxes(t, 0, 1) for t in (r, decay, k, v, a_vec, b_vec))
    s_fin, ys = lax.scan(step, s0.astype(f32), seq)
    y = jnp.swapaxes(ys, 0, 1)
    mean = jnp.mean(y, axis=-1, keepdims=True)
    var = jnp.mean(jnp.square(y - mean), axis=-1, keepdims=True)
    y = (y - mean) * lax.rsqrt(var + LNX_EPS) * lnx_g.astype(f32).reshape(H_A, HD_A) + lnx_b.astype(f32).reshape(H_A, HD_A)
    y = y + jnp.sum(r * k * r_k.astype(f32), axis=-1, keepdims=True) * v
    return y.reshape(B, T, D_A), s_fin.astype(s0.dtype), cols[:, -1]


def fox_attention(q, k, v, logf, q_off):
    B, T, H, D = q.shape
    S = k.shape[1]
    c = jnp.cumsum(logf.astype(jnp.float32), axis=1)
    c_keys = jnp.swapaxes(c, 1, 2)
    c_q = c[:, q_off:]
    qb = min(Q_BLOCK, T)
    nb = -(-T // qb)
    pad = nb * qb - T
    q_blk = jnp.swapaxes(jnp.pad(q, ((0, 0), (0, pad), (0, 0), (0, 0))).reshape(B, nb, qb, H, D), 0, 1)
    c_blk = jnp.swapaxes(jnp.pad(c_q, ((0, 0), (0, pad), (0, 0))).reshape(B, nb, qb, H), 0, 1)
    k_pos = jnp.arange(S)
    scale = D ** -0.5

    def one_block(args):
        i, qi, ci = args
        q_pos = q_off + i * qb + jnp.arange(qb)
        s = jnp.einsum('bqhd,bkhd->bhqk', qi, k, preferred_element_type=jnp.float32) * scale
        s = s + jnp.swapaxes(ci, 1, 2)[..., None] - c_keys[:, :, None, :]
        s = jnp.where(k_pos[None, :] <= q_pos[:, None], s, -jnp.inf)
        p = jax.nn.softmax(s, axis=-1)
        return jnp.einsum('bhqk,bkhd->bqhd', p.astype(v.dtype), v)

    out = lax.map(one_block, (jnp.arange(nb), q_blk, c_blk))
    return jnp.swapaxes(out, 0, 1).reshape(B, nb * qb, H, D)[:, :T]


def rotary(x, pos):
    half = x.shape[-1] // 2
    inv = ROPE_BASE ** (-jnp.arange(half, dtype=jnp.float32) / half)
    ang = pos.astype(jnp.float32)[:, None] * inv[None, :]
    cos = jnp.cos(ang)[None, :, None, :]
    sin = jnp.sin(ang)[None, :, None, :]
    xf = x.astype(jnp.float32)
    x1, x2 = xf[..., :half], xf[..., half:]
    return jnp.concatenate([x1 * cos - x2 * sin, x1 * sin + x2 * cos], axis=-1)


def retention_chunkwise(q, k, v, s0, chunk):
    f32 = jnp.float32
    B, T, H, _ = q.shape
    DV = v.shape[-1]
    nc = T // chunk
    log_g = jnp.log1p(-jnp.exp2(-5.0 - jnp.arange(H, dtype=f32)))
    n = jnp.arange(chunk, dtype=f32)
    diff = n[:, None] - n[None, :]
    dmask = jnp.where(diff >= 0, jnp.exp(jnp.maximum(diff, 0.0)[None] * log_g[:, None, None]), 0.0)
    q_dec = jnp.exp((n + 1.0)[:, None] * log_g[None, :])
    k_dec = jnp.exp((chunk - 1.0 - n)[:, None] * log_g[None, :])
    c_dec = jnp.exp(chunk * log_g)
    to_chunks = lambda t: jnp.swapaxes(t.astype(f32).reshape(B, nc, chunk, H, t.shape[-1]), 0, 1)

    def step(S, inp):
        qc, kc, vc = inp
        att = jnp.einsum('bnhd,bmhd->bhnm', qc, kc) * dmask
        inner = jnp.einsum('bhnm,bmhe->bnhe', att, vc)
        cross = jnp.einsum('bnhd,bhde->bnhe', qc * q_dec[None, :, :, None], S)
        S = S * c_dec[None, :, None, None] + jnp.einsum('bmhd,bmhe->bhde', kc * k_dec[None, :, :, None], vc)
        return S, inner + cross

    s_fin, o = lax.scan(step, s0.astype(f32), (to_chunks(q), to_chunks(k), to_chunks(v)))
    return jnp.swapaxes(o, 0, 1).reshape(B, T, H, DV), s_fin


def ab_layer(h, wkv0, shift0, past_k, past_v, past_logf, norm_g, w_in, mu, w0, w2, a0, a2,
             k_k, k_a, r_k, lnx_g, lnx_b, b_forget, w_out):
    B, T, _ = h.shape
    z = rms_norm(h, norm_g) @ w_in
    c1 = SHIFT_DIM
    c2 = c1 + D_A
    c3 = c2 + D_B
    c4 = c3 + D_B
    c5 = c4 + D_B
    c6 = c5 + H_B
    cols, gate_a, q, k, v, f, gate_b = jnp.split(z, [c1, c2, c3, c4, c5, c6], axis=-1)
    y_a, wkv, last_row = rwkv7_mix(cols, shift0, wkv0, mu, w0, w2, a0, a2, k_k, k_a, r_k, lnx_g, lnx_b)
    q = q.reshape(B, T, H_B, HD_B)
    k = k.reshape(B, T, H_B, HD_B)
    v = v.reshape(B, T, H_B, HD_B)
    logf = jax.nn.log_sigmoid(f.astype(jnp.float32) + b_forget.astype(jnp.float32))
    if past_k is None:
        k_all, v_all, logf_all, p = k, v, logf, 0
    else:
        k_all = jnp.concatenate([past_k.astype(k.dtype), k], axis=1)
        v_all = jnp.concatenate([past_v.astype(v.dtype), v], axis=1)
        logf_all = jnp.concatenate([past_logf.astype(jnp.float32), logf], axis=1)
        p = past_k.shape[1]
    y_b = fox_attention(q, k_all, v_all, logf_all, p)
    mixed = jnp.concatenate([y_a.astype(h.dtype) * jax.nn.silu(gate_a),
                             y_b.reshape(B, T, D_B) * jax.nn.silu(gate_b)], axis=-1)
    return h + mixed @ w_out, wkv, last_row, k, v, logf


def retention_layer(h, s0, pos0, lead, chunk, norm_g, w_in, ret_norm_g, w_out):
    B, T, _ = h.shape
    z = rms_norm(h, norm_g) @ w_in
    q, k, v, g = jnp.split(z, [DQK_C, 2 * DQK_C, 2 * DQK_C + DVW_C], axis=-1)
    pos = pos0 + jnp.arange(T)
    q = rotary(q.reshape(B, T, H_C, DK_C), pos) * (DK_C ** -0.5)
    k = rotary(k.reshape(B, T, H_C, DK_C), pos)
    v = v.reshape(B, T, H_C, DV_C)
    if lead > 0:
        o1, s_mid = retention_chunkwise(q[:, :lead], k[:, :lead], v[:, :lead], s0, lead)
        o2, s_fin = retention_chunkwise(q[:, lead:], k[:, lead:], v[:, lead:], s_mid, chunk)
        o = jnp.concatenate([o1, o2], axis=1)
    else:
        o, s_fin = retention_chunkwise(q, k, v, s0, chunk)
    o = o * lax.rsqrt(jnp.mean(o * o, axis=-1, keepdims=True) + RMS_EPS) * ret_norm_g.astype(jnp.float32).reshape(H_C, DV_C)
    y = (jax.nn.silu(g) * o.reshape(B, T, DVW_C).astype(h.dtype)) @ w_out
    return h + y, s_fin.astype(s0.dtype)


def setup_inputs(seed: int = 0) -> dict:
    key = jax.random.key(seed)
    ks = jax.random.split(key, 32)
    f32 = jnp.float32
    nrm = lambda kk, shape, s=1.0: s * jax.random.normal(kk, shape, f32)
    n_pages = PAST_LEN // PAGE_SIZE
    n_used = DEC_BATCH * n_pages
    n_pool = n_used + (n_used + 3) // 4
    page_table = jax.random.permutation(ks[0], n_pool)[:n_used].reshape(DEC_BATCH, n_pages).astype(jnp.int32)
    return {
        'x_prompt': nrm(ks[1], (BATCH, SEQ, D_MODEL)),
        'x_sample': nrm(ks[2], (DEC_BATCH, DEC_SEQ, D_MODEL)),
        'state_wkv': nrm(ks[3], (DEC_BATCH, H_A, HD_A, HD_A), 0.3),
        'state_shift': nrm(ks[4], (DEC_BATCH, SHIFT_DIM)),
        'cache_fox_k': nrm(ks[5], (n_pool, PAGE_SIZE, H_B, HD_B)),
        'cache_fox_v': nrm(ks[6], (n_pool, PAGE_SIZE, H_B, HD_B)),
        'cache_fox_logf': jax.nn.log_sigmoid(3.0 + nrm(ks[7], (n_pool, PAGE_SIZE, H_B))),
        'state_ret': nrm(ks[8], (DEC_BATCH, H_C, DK_C, DV_C), 4.0),
        'page_table': page_table,
        'meta_tokens': nrm(ks[9], (N_META, D_MODEL)),
        'norm0_g': 1.0 + nrm(ks[10], (D_MODEL,), 0.02),
        'w_in0': nrm(ks[11], (D_MODEL, N_IN0), D_MODEL ** -0.5),
        'mu_shift': jax.random.uniform(ks[12], (SHIFT_DIM,), f32),
        'w0_decay': -2.5 + nrm(ks[13], (D_A,), 1.0),
        'w2_decay': nrm(ks[14], (LORA_W, D_A), 0.5 * LORA_W ** -0.5),
        'a0_iclr': nrm(ks[15], (D_A,), 0.5),
        'a2_iclr': nrm(ks[16], (LORA_A, D_A), 0.5 * LORA_A ** -0.5),
        'k_k': 1.0 + nrm(ks[17], (D_A,), 0.1),
        'k_a': 1.0 + nrm(ks[18], (D_A,), 0.1),
        'r_k': nrm(ks[19], (H_A, HD_A), 0.1),
        'lnx_g': 1.0 + nrm(ks[20], (D_A,), 0.02),
        'lnx_b': nrm(ks[21], (D_A,), 0.02),
        'b_forget': 3.0 + nrm(ks[22], (H_B,), 0.5),
        'w_out0': nrm(ks[23], (D_A + D_B, D_MODEL), (D_A + D_B) ** -0.5),
        'norm1_g': 1.0 + nrm(ks[24], (D_MODEL,), 0.02),
        'w_in1': nrm(ks[25], (D_MODEL, N_IN1), D_MODEL ** -0.5),
        'ret_norm_g': 1.0 + nrm(ks[26], (DVW_C,), 0.02),
        'w_out1': nrm(ks[27], (DVW_C, D_MODEL), DVW_C ** -0.5),
        'final_norm_g': 1.0 + nrm(ks[28], (D_MODEL,), 0.02),
    }


def reference(x_prompt, x_sample, state_wkv, state_shift, cache_fox_k, cache_fox_v, cache_fox_logf, state_ret,
              page_table, meta_tokens, norm0_g, w_in0, mu_shift, w0_decay, w2_decay, a0_iclr, a2_iclr, k_k, k_a,
              r_k, lnx_g, lnx_b, b_forget, w_out0, norm1_g, w_in1, ret_norm_g, w_out1, final_norm_g):
    B = x_prompt.shape[0]
    DB = x_sample.shape[0]
    dt = x_prompt.dtype
    hp = jnp.concatenate([jnp.broadcast_to(meta_tokens.astype(dt)[None], (B, N_META, D_MODEL)), x_prompt], axis=1)
    hs = x_sample
    n_pages = page_table.shape[1]
    past_len = n_pages * PAGE_SIZE
    past_k = cache_fox_k[page_table].reshape(DB, past_len, H_B, HD_B)
    past_v = cache_fox_v[page_table].reshape(DB, past_len, H_B, HD_B)
    past_logf = cache_fox_logf[page_table].reshape(DB, past_len, H_B)
    ab_w = (norm0_g, w_in0, mu_shift, w0_decay, w2_decay, a0_iclr, a2_iclr, k_k, k_a, r_k, lnx_g, lnx_b, b_forget, w_out0)
    c_w = (norm1_g, w_in1, ret_norm_g, w_out1)
    for layer in range(DEPTH):
        if layer % 2 == 0:
            hp, p_wkv, p_shift, p_k, p_v, p_logf = ab_layer(
                hp, jnp.zeros((B, H_A, HD_A, HD_A), jnp.float32), jnp.zeros((B, SHIFT_DIM), dt),
                None, None, None, *ab_w)
            hs, s_wkv, s_shift, s_k, s_v, s_logf = ab_layer(
                hs, state_wkv, state_shift, past_k, past_v, past_logf, *ab_w)
        else:
            hp, p_ret = retention_layer(hp, jnp.zeros((B, H_C, DK_C, DV_C), jnp.float32), 0, N_META, RET_CHUNK, *c_w)
            hs, s_ret = retention_layer(hs, state_ret, past_len, 0, hs.shape[1], *c_w)
    y_prompt = rms_norm(hp, final_norm_g)[:, N_META:]
    y_sample = rms_norm(hs, final_norm_g)
    return (y_prompt, y_sample, p_wkv, p_shift, p_k, p_v, p_logf, p_ret, s_wkv, s_shift, s_k, s_v, s_logf, s_ret)
```

```python
import functools

import numpy as np
import jax
import jax.numpy as jnp
from jax import lax
from jax.experimental import pallas as pl
from jax.experimental.pallas import tpu as pltpu

F32 = jnp.float32
BF16 = jnp.bfloat16

D_MODEL = 2048
N_META = 16
HD_A = 64
H_A = 16
D_A = 1024
LORA = 64
SHIFT_DIM = 3 * D_A + 2 * LORA
HD_B = 128
H_B = 8
D_B = 1024
H_C = 8
DK_C = 128
DV_C = 256
DQK_C = H_C * DK_C
DVW_C = H_C * DV_C
N_IN1 = 2 * DQK_C + 2 * DVW_C
PAGE = 128
RMS_EPS = 1e-6
LNX_EPS = 64e-5
ROPE_BASE = 10000.0

LANES = 128
SUBLANES = 8
VMEM_LIMIT = 56 * 1024 * 1024

ZC_W = SHIFT_DIM + LANES
ZB_W = 5 * D_B
ZB_GATE_A, ZB_Q, ZB_K, ZB_V, ZB_GATE_B = 0, 1, 2, 3, 4

RWKV_CHUNK = 64
RET_CHUNK = 128
ROWS_S = 16
PAGES_PER_STEP = 4


def _cparams(sem):
    return pltpu.CompilerParams(dimension_semantics=sem, vmem_limit_bytes=VMEM_LIMIT)


def _dot(a, b):
    return jnp.dot(a.astype(BF16), b.astype(BF16), preferred_element_type=F32)


def _dot_nt(a, b):
    return lax.dot_general(a.astype(BF16), b.astype(BF16), (((1,), (1,)), ((), ())),
                           preferred_element_type=F32)


def _dot_tn(a, b):
    return jnp.dot(a.T.astype(BF16), b.astype(BF16), preferred_element_type=F32)


def _split3(x):
    hi = x.astype(BF16)
    r1 = x - hi.astype(F32)
    mid = r1.astype(BF16)
    lo = (r1 - mid.astype(F32)).astype(BF16)
    return hi, mid, lo


def _dot_sel(x, sel):
    hi, mid, lo = _split3(x)
    f = lambda p: jnp.dot(p, sel, preferred_element_type=F32)
    return f(hi) + f(mid) + f(lo)


def _sel_dot(sel, x):
    hi, mid, lo = _split3(x)
    f = lambda p: jnp.dot(sel, p, preferred_element_type=F32)
    return f(hi) + f(mid) + f(lo)


def _sel_dot_nt(sel, x):
    hi, mid, lo = _split3(x)
    f = lambda p: lax.dot_general(sel, p, (((1,), (1,)), ((), ())), preferred_element_type=F32)
    return f(hi) + f(mid) + f(lo)


def _mask01(m):
    return jnp.where(m, 1.0, 0.0).astype(BF16)


def _sigmoid(x):
    return 1.0 / (1.0 + jnp.exp(-x))


def _silu(x):
    return x * _sigmoid(x)


def _softplus(x):
    return jnp.maximum(x, 0.0) + jnp.log(1.0 + jnp.exp(-jnp.abs(x)))


def _pad_rows(x, rows):
    if x.shape[0] == rows:
        return x
    return jnp.concatenate([x, jnp.zeros((rows - x.shape[0], x.shape[1]), x.dtype)], axis=0)


def _row_block(m):
    for t in (1664, 1024, 640, 512, 256, 128, 64, 32, 16):
        if m % t == 0:
            return t
    raise ValueError(m)


def _const_spec(shape, n_grid):
    zeros = tuple(0 for _ in shape)
    if n_grid == 1:
        return pl.BlockSpec(shape, lambda i: zeros)
    return pl.BlockSpec(shape, lambda i, j: zeros)


def _norm_kernel(x_ref, g_ref, o_ref):
    x = x_ref[...]
    y = x * lax.rsqrt(jnp.mean(x * x, axis=-1, keepdims=True) + RMS_EPS)
    o_ref[...] = (y * g_ref[...]).astype(o_ref.dtype)


def rms_norm_rows(x, g, out_dtype, tm, drop_blocks=0):
    m, d = x.shape
    return pl.pallas_call(
        _norm_kernel,
        grid=(m // tm,),
        in_specs=[pl.BlockSpec((tm, d), lambda i: (i, 0)), _const_spec((1, d), 1)],
        out_specs=pl.BlockSpec((tm, d), lambda i: (jnp.maximum(i - drop_blocks, 0), 0)),
        out_shape=jax.ShapeDtypeStruct((m - drop_blocks * tm, d), out_dtype),
        compiler_params=_cparams(("arbitrary",)),
        name="rms_norm",
    )(x, g.reshape(1, d).astype(F32))


def _mm_kernel(x_ref, w_ref, o_ref):
    o_ref[...] = jnp.dot(x_ref[...], w_ref[...], preferred_element_type=F32)


def _mm_res_kernel(x_ref, w_ref, r_ref, o_ref):
    o_ref[...] = r_ref[...] + jnp.dot(x_ref[...], w_ref[...], preferred_element_type=F32)


def _mm2_res_kernel(xa_ref, xb_ref, w_ref, r_ref, o_ref):
    ka = xa_ref.shape[1]
    acc = jnp.dot(xa_ref[...], w_ref[:ka, :], preferred_element_type=F32)
    acc += jnp.dot(xb_ref[...], w_ref[ka:, :], preferred_element_type=F32)
    o_ref[...] = r_ref[...] + acc


def matmul(xs, w, res=None):
    m = xs[0].shape[0]
    k, n = w.shape
    tm = _row_block(m)
    tn = 512 if n % 512 == 0 else 256
    assert n % tn == 0 and sum(x.shape[1] for x in xs) == k
    in_specs = [pl.BlockSpec((tm, x.shape[1]), lambda i, j: (i, 0)) for x in xs]
    in_specs.append(pl.BlockSpec((k, tn), lambda i, j: (0, j)))
    args = list(xs) + [w]
    if res is not None:
        in_specs.append(pl.BlockSpec((tm, tn), lambda i, j: (i, j)))
        args.append(res)
        body = _mm_res_kernel if len(xs) == 1 else _mm2_res_kernel
    else:
        assert len(xs) == 1
        body = _mm_kernel
    return pl.pallas_call(
        body,
        grid=(m // tm, n // tn),
        in_specs=in_specs,
        out_specs=pl.BlockSpec((tm, tn), lambda i, j: (i, j)),
        out_shape=jax.ShapeDtypeStruct((m, n), F32),
        compiler_params=_cparams(("arbitrary", "arbitrary")),
        name="matmul",
    )(*args)


def _rwkv_kernel(zc_ref, ga_ref, shift0_ref, s0_ref, mu_ref, w0_ref, a0_ref, lora_ref, kk_ref, ka_ref,
                 rk_ref, lng_ref, lnb_ref, seg_ref, segt_ref,
                 y_ref, sout_ref, shiftout_ref,
                 s_scr, carry_scr, *, chunk, t_valid, n_chunks):
    C = chunk
    log2c = C.bit_length() - 1
    assert C == 1 << log2c
    ci = pl.program_id(1)

    @pl.when(ci == 0)
    def _():
        s_scr[...] = s0_ref[0]
        carry_scr[...] = shift0_ref[0]

    c = zc_ref[...]
    row = lax.broadcasted_iota(jnp.int32, (C, 1), 0)
    prev = jnp.where(row == 0, carry_scr[...], pltpu.roll(c, 1, axis=0))
    carry_scr[...] = c[C - 1:C, :]
    xs = c + (prev - c) * mu_ref[...]
    r = xs[:, :D_A]
    k = xs[:, D_A:2 * D_A]
    v = xs[:, 2 * D_A:3 * D_A]
    lo = xs[:, 3 * D_A:]
    lane = lax.broadcasted_iota(jnp.int32, (1, LANES), 1)
    lo = jnp.where(lane < LORA, jnp.tanh(lo), lo)
    proj = _dot(lo, lora_ref[...])
    w_log = -_softplus(-(w0_ref[...] + proj[:, :D_A])) - 0.5
    ell = -jnp.exp(w_log)
    asig = _sigmoid(a0_ref[...] + proj[:, D_A:])

    seg = seg_ref[...]
    segt = segt_ref[...]
    head_sum = lambda t: _dot_sel(_dot_sel(t, seg), segt)

    kkv = k * kk_ref[...]
    kkv = kkv * lax.rsqrt(head_sum(kkv * kkv) + 1e-12)
    kmod = k * (1.0 + (asig - 1.0) * ka_ref[...])
    a_vec = -kkv
    b_vec = kkv * asig
    if t_valid % C:
        ok = (ci * C + row) < t_valid
        ell = jnp.where(ok, ell, 0.0)
        a_vec = jnp.where(ok, a_vec, 0.0)
        b_vec = jnp.where(ok, b_vec, 0.0)
        kmod = jnp.where(ok, kmod, 0.0)
        v = jnp.where(ok, v, 0.0)

    ti = lax.broadcasted_iota(jnp.int32, (C, C), 0)
    si = lax.broadcasted_iota(jnp.int32, (C, C), 1)
    cum = _sel_dot(_mask01(si <= ti), ell)
    e_in = jnp.exp(cum)
    e_ex = jnp.exp(cum - ell)
    e_inv = jnp.exp(-cum)
    g_end = e_in[C - 1:C, :]
    a_t = a_vec * e_ex
    r_t = r * e_in
    k_t = kmod * e_inv
    b_t = b_vec * e_inv
    k_h = k_t * g_end
    b_h = b_t * g_end

    first = lane < HD_A
    t2 = lax.broadcasted_iota(jnp.int32, (2 * C, 2 * C), 0)
    s2 = lax.broadcasted_iota(jnp.int32, (2 * C, 2 * C), 1)
    same = (t2 >> log2c) == (s2 >> log2c)
    tin = t2 & (C - 1)
    sin = s2 & (C - 1)
    strict = jnp.logical_and(same, sin < tin)
    incl = jnp.logical_and(same, sin <= tin)
    eye = jnp.where(t2 == s2, 1.0, 0.0)

    def stack(x, g):
        xg = x[:, g * LANES:(g + 1) * LANES]
        return jnp.concatenate([jnp.where(first, xg, 0.0), jnp.where(first, 0.0, xg)], axis=0)

    ys = []
    for g in range(D_A // LANES):
        ast, rst, bst, kst = stack(a_t, g), stack(r_t, g), stack(b_t, g), stack(k_t, g)
        bhs, khs, vst = stack(b_h, g), stack(k_h, g), stack(v, g)
        x_ab = jnp.where(strict, _dot_nt(ast, bst), 0.0)
        x_ak = jnp.where(strict, _dot_nt(ast, kst), 0.0)
        x_rb = jnp.where(incl, _dot_nt(rst, bst), 0.0)
        x_rk = jnp.where(incl, _dot_nt(rst, kst), 0.0)
        tinv = eye + x_ab
        xp = x_ab
        n = 2
        while n < C:
            xp = _dot(xp, xp)
            tinv = tinv + _dot(tinv, xp)
            n *= 2
        ta = _dot(tinv, jnp.concatenate([ast, _dot(x_ak, vst)], axis=1))
        a_hat = ta[:, :LANES]
        u0 = ta[:, LANES:]
        s_g = s_scr[g]
        ar = _dot_nt(jnp.concatenate([a_hat, rst], axis=0), s_g)
        u = ar[:2 * C] + u0
        yst = ar[2 * C:] + _dot(x_rb, u) + _dot(x_rk, vst)
        ys.append(yst[:C] + yst[C:])
        uv = jnp.concatenate([u, vst], axis=0)
        bk = jnp.concatenate([bhs, khs], axis=0)
        s_scr[g] = s_g * g_end[:, g * LANES:(g + 1) * LANES] + _dot_tn(uv, bk)

    y = jnp.concatenate(ys, axis=1)
    inv_hd = 1.0 / HD_A
    mean = head_sum(y) * inv_hd
    d = y - mean
    var = head_sum(d * d) * inv_hd
    y = d * lax.rsqrt(var + LNX_EPS) * lng_ref[...] + lnb_ref[...]
    y = y + head_sum(r * kmod * rk_ref[...]) * v
    y_ref[...] = (y * _silu(ga_ref[...])).astype(y_ref.dtype)

    @pl.when(ci == n_chunks - 1)
    def _():
        sout_ref[0] = s_scr[...]
        last = (t_valid - 1) % C
        shiftout_ref[0] = c[last:last + 1, :]


def _pair_state(s):
    b = s.shape[0]
    s = s.reshape(b, H_A // 2, 2, HD_A, HD_A)
    z = jnp.zeros_like(s[:, :, 0])
    top = jnp.concatenate([s[:, :, 0], z], axis=-1)
    bot = jnp.concatenate([z, s[:, :, 1]], axis=-1)
    return jnp.concatenate([top, bot], axis=-2)


def _unpair_state(s):
    b = s.shape[0]
    s = s.reshape(b, H_A // 2, 2, HD_A, 2, HD_A)
    return jnp.stack([s[:, :, 0, :, 0, :], s[:, :, 1, :, 1, :]], axis=2).reshape(b, H_A, HD_A, HD_A)


def rwkv_mix(zc, gate_a_src, gate_blk, n_batch, t_pad, t_valid, shift0, s0, wts):
    C = RWKV_CHUNK
    assert t_pad % C == 0
    n_chunks = t_pad // C
    row = lambda a: a.reshape(1, -1).astype(F32)
    lora = jnp.zeros((LANES, 2 * D_A), F32)
    lora = lora.at[:LORA, :D_A].set(wts["w2_decay"]).at[LORA:, D_A:].set(wts["a2_iclr"])
    head_of = np.arange(D_A) // HD_A
    seg = (head_of[:, None] == np.arange(LANES)[None, :]).astype(np.float32)
    cs = lambda shape: _const_spec(shape, 2)
    kern = functools.partial(_rwkv_kernel, chunk=C, t_valid=t_valid, n_chunks=n_chunks)
    y, s_out, shift_out = pl.pallas_call(
        kern,
        grid=(n_batch, n_chunks),
        in_specs=[
            pl.BlockSpec((C, SHIFT_DIM), lambda b, i: (b * n_chunks + i, 0)),
            pl.BlockSpec((C, D_A), lambda b, i: (b * n_chunks + i, gate_blk)),
            pl.BlockSpec((1, 1, SHIFT_DIM), lambda b, i: (b, 0, 0)),
            pl.BlockSpec((1, H_A // 2, LANES, LANES), lambda b, i: (b, 0, 0, 0)),
            cs((1, SHIFT_DIM)), cs((1, D_A)), cs((1, D_A)), cs((LANES, 2 * D_A)),
            cs((1, D_A)), cs((1, D_A)), cs((1, D_A)), cs((1, D_A)), cs((1, D_A)),
            cs((D_A, LANES)), cs((LANES, D_A)),
        ],
        out_specs=[
            pl.BlockSpec((C, D_A), lambda b, i: (b * n_chunks + i, 0)),
            pl.BlockSpec((1, H_A // 2, LANES, LANES), lambda b, i: (b, 0, 0, 0)),
            pl.BlockSpec((1, 1, SHIFT_DIM), lambda b, i: (b, 0, 0)),
        ],
        out_shape=[
            jax.ShapeDtypeStruct((n_batch * t_pad, D_A), BF16),
            jax.ShapeDtypeStruct((n_batch, H_A // 2, LANES, LANES), F32),
            jax.ShapeDtypeStruct((n_batch, 1, SHIFT_DIM), F32),
        ],
        scratch_shapes=[pltpu.VMEM((H_A // 2, LANES, LANES), F32), pltpu.VMEM((1, SHIFT_DIM), F32)],
        compiler_params=_cparams(("arbitrary", "arbitrary")),
        name="rwkv7_mix",
    )(zc, gate_a_src, shift0.reshape(n_batch, 1, SHIFT_DIM).astype(F32), _pair_state(s0.astype(F32)),
      row(wts["mu_shift"]), row(wts["w0_decay"]), row(wts["a0_iclr"]), lora,
      row(wts["k_k"]), row(wts["k_a"]), row(wts["r_k"]), row(wts["lnx_g"]), row(wts["lnx_b"]),
      jnp.asarray(seg, BF16), jnp.asarray(seg.T, BF16))
    return y, _unpair_state(s_out), shift_out.reshape(n_batch, SHIFT_DIM)


def _logf_kernel(f_ref, bias_ref, logf_ref, cum_t_ref, carry_scr, *, tb):
    @pl.when(pl.program_id(0) == 0)
    def _():
        carry_scr[...] = jnp.zeros_like(carry_scr)

    x = f_ref[...] + bias_ref[...]
    logf = jnp.minimum(x, 0.0) - jnp.log(1.0 + jnp.exp(-jnp.abs(x)))
    logf_ref[...] = logf
    ti = lax.broadcasted_iota(jnp.int32, (tb, tb), 0)
    si = lax.broadcasted_iota(jnp.int32, (tb, tb), 1)
    cum = _sel_dot(_mask01(si <= ti), logf) + carry_scr[...]
    carry_scr[...] = cum[tb - 1:tb, :]
    cum_t_ref[...] = cum.T[:SUBLANES, :]


def forget_logs(zc, b_forget):
    m = zc.shape[0]
    tb = LANES
    bias = jnp.zeros((1, LANES), F32).at[0, :H_B].set(b_forget.astype(F32))
    return pl.pallas_call(
        functools.partial(_logf_kernel, tb=tb),
        grid=(m // tb,),
        in_specs=[pl.BlockSpec((tb, LANES), lambda i: (i, SHIFT_DIM // LANES)), _const_spec((1, LANES), 1)],
        out_specs=[pl.BlockSpec((tb, LANES), lambda i: (i, 0)),
                   pl.BlockSpec((SUBLANES, tb), lambda i: (0, i))],
        out_shape=[jax.ShapeDtypeStruct((m, LANES), F32), jax.ShapeDtypeStruct((SUBLANES, m), F32)],
        scratch_shapes=[pltpu.VMEM((1, LANES), F32)],
        compiler_params=_cparams(("arbitrary",)),
        name="forget_logs",
    )(zc, bias)


def _fox_prefill_kernel(q_ref, k_ref, v_ref, gate_ref, cum_ref, o_ref, m_scr, l_scr, acc_scr, *, tq, pad):
    qi = pl.program_id(0)
    ki = pl.program_id(1)

    @pl.when(ki == 0)
    def _():
        m_scr[...] = jnp.full_like(m_scr, -jnp.inf)
        l_scr[...] = jnp.zeros_like(l_scr)
        acc_scr[...] = jnp.zeros_like(acc_scr)

    @pl.when(ki <= qi)
    def _():
        q_pos = qi * tq + lax.broadcasted_iota(jnp.int32, (tq, tq), 0)
        k_pos = ki * tq + lax.broadcasted_iota(jnp.int32, (tq, tq), 1)
        keep = jnp.logical_and(k_pos <= q_pos, jnp.logical_or(k_pos >= pad, q_pos < pad))
        scale = HD_B ** -0.5
        for h in range(H_B):
            sl = slice(h * HD_B, (h + 1) * HD_B)
            s = _dot_nt(q_ref[:, sl] * scale, k_ref[:, sl]) - cum_ref[h:h + 1, :]
            s = jnp.where(keep, s, -jnp.inf)
            m_old = m_scr[h]
            m_new = jnp.maximum(m_old, jnp.max(s, axis=-1, keepdims=True))
            alpha = jnp.exp(m_old - m_new)
            p = jnp.exp(s - m_new)
            l_scr[h] = alpha * l_scr[h] + jnp.sum(p, axis=-1, keepdims=True)
            acc_scr[:, sl] = alpha * acc_scr[:, sl] + _dot(p, v_ref[:, sl])
            m_scr[h] = m_new

    @pl.when(ki == qi)
    def _():
        for h in range(H_B):
            sl = slice(h * HD_B, (h + 1) * HD_B)
            y = acc_scr[:, sl] / l_scr[h]
            o_ref[:, sl] = (y * _silu(gate_ref[:, sl])).astype(o_ref.dtype)


def fox_prefill(zb, cum_t, pad, tq):
    m = zb.shape[0]
    nq = m // tq
    kv = lambda blk: pl.BlockSpec((tq, D_B), lambda qi, ki: (jnp.minimum(ki, qi), blk))
    return pl.pallas_call(
        functools.partial(_fox_prefill_kernel, tq=tq, pad=pad),
        grid=(nq, nq),
        in_specs=[
            pl.BlockSpec((tq, D_B), lambda qi, ki: (qi, ZB_Q)),
            kv(ZB_K), kv(ZB_V),
            pl.BlockSpec((tq, D_B), lambda qi, ki: (qi, ZB_GATE_B)),
            pl.BlockSpec((SUBLANES, tq), lambda qi, ki: (0, jnp.minimum(ki, qi))),
        ],
        out_specs=pl.BlockSpec((tq, D_B), lambda qi, ki: (qi, 0)),
        out_shape=jax.ShapeDtypeStruct((m, D_B), BF16),
        scratch_shapes=[pltpu.VMEM((H_B, tq, 1), F32), pltpu.VMEM((H_B, tq, 1), F32),
                        pltpu.VMEM((tq, D_B), F32)],
        compiler_params=_cparams(("arbitrary", "arbitrary")),
        name="fox_prefill",
    )(zb, zb, zb, zb, cum_t)


def _fox_decode_kernel(pt_ref, q_ref, kn_ref, vn_ref, gate_ref, lfn_ref, *rest, n_new, n_steps):
    pp = PAGES_PER_STEP
    k_refs, v_refs, lf_refs = rest[:pp], rest[pp:2 * pp], rest[2 * pp:3 * pp]
    o_ref = rest[3 * pp]
    qst_scr, m_scr, l_scr, acc_scr, off_scr = rest[3 * pp + 1:]
    del pt_ref
    j = pl.program_id(1)
    R = ROWS_S
    rows = H_B * R
    lane_head = lax.broadcasted_iota(jnp.int32, (1, D_B), 1) >> (HD_B.bit_length() - 1)

    def online(s, v_bf):
        m_old = m_scr[...]
        m_new = jnp.maximum(m_old, jnp.max(s, axis=-1, keepdims=True))
        alpha = jnp.exp(m_old - m_new)
        p = jnp.exp(s - m_new)
        l_scr[...] = alpha * l_scr[...] + jnp.sum(p, axis=-1, keepdims=True)
        acc_scr[...] = alpha * acc_scr[...] + jnp.dot(p.astype(BF16), v_bf, preferred_element_type=F32)
        m_scr[...] = m_new

    @pl.when(j == 0)
    def _():
        q = q_ref[...] * (HD_B ** -0.5)
        qst = jnp.concatenate([jnp.where(lane_head == h, q, 0.0) for h in range(H_B)], axis=0)
        qst_scr[...] = qst.astype(BF16)
        m_scr[...] = jnp.full_like(m_scr, -jnp.inf)
        l_scr[...] = jnp.zeros_like(l_scr)
        acc_scr[...] = jnp.zeros_like(acc_scr)
        off_scr[...] = jnp.zeros_like(off_scr)
        lf = lfn_ref[...]
        trow = lax.broadcasted_iota(jnp.int32, (R, 1), 0)
        cn = lf
        sh = 1
        while sh < R:
            cn = cn + jnp.where(trow >= sh, pltpu.roll(cn, sh, axis=0), 0.0)
            sh *= 2
        r_i = lax.broadcasted_iota(jnp.int32, (rows, LANES), 0)
        c_i = lax.broadcasted_iota(jnp.int32, (rows, LANES), 1)
        pick = _mask01(c_i == (r_i >> (R.bit_length() - 1)))
        c_keys = _sel_dot_nt(pick, _pad_rows(cn, LANES))
        s = _dot_nt(qst_scr[...], _pad_rows(kn_ref[...], LANES)) - c_keys
        s = jnp.where(jnp.logical_and(c_i <= (r_i & (R - 1)), c_i < n_new), s, -jnp.inf)
        online(s, _pad_rows(vn_ref[...], LANES).astype(BF16))

    k_i = lax.broadcasted_iota(jnp.int32, (PAGE, PAGE), 0)
    k_j = lax.broadcasted_iota(jnp.int32, (PAGE, PAGE), 1)
    later = _mask01(k_i > k_j)
    for i in range(pp):
        lf_t = lf_refs[i][0]
        lf_rows = jnp.concatenate([jnp.broadcast_to(lf_t[h:h + 1, :], (R, PAGE)) for h in range(H_B)], axis=0)
        bias = _dot_sel(lf_rows, later) + off_scr[...]
        s = _dot_nt(qst_scr[...], k_refs[i][0]) + bias
        online(s, v_refs[i][0].astype(BF16))
        off_scr[...] = off_scr[...] + jnp.sum(lf_rows, axis=-1, keepdims=True)

    @pl.when(j == n_steps - 1)
    def _():
        inv = 1.0 / l_scr[...]
        y = jnp.concatenate([acc_scr[h * R:(h + 1) * R, h * HD_B:(h + 1) * HD_B] * inv[h * R:(h + 1) * R]
                             for h in range(H_B)], axis=1)
        o_ref[...] = (y * _silu(gate_ref[...])).astype(o_ref.dtype)


def fox_decode(zb, logf_new, page_table, cache_k, cache_v, cache_logf, n_new):
    n_batch, n_pages = page_table.shape
    n_pool = cache_k.shape[0]
    pp = PAGES_PER_STEP
    assert n_pages % pp == 0
    n_steps = n_pages // pp
    ck = cache_k.reshape(n_pool, PAGE, D_B)
    cv = cache_v.reshape(n_pool, PAGE, D_B)
    clf = jnp.swapaxes(cache_logf, 1, 2)
    rowblk = lambda blk: pl.BlockSpec((ROWS_S, D_B), lambda b, j, pt: (b, blk))

    def page_spec(shape, i):
        return pl.BlockSpec(shape, lambda b, j, pt: (pt[b, n_pages - 1 - (j * pp + i)], 0, 0))

    in_specs = [rowblk(ZB_Q), rowblk(ZB_K), rowblk(ZB_V), rowblk(ZB_GATE_B),
                pl.BlockSpec((ROWS_S, LANES), lambda b, j, pt: (b, 0))]
    in_specs += [page_spec((1, PAGE, D_B), i) for i in range(pp)]
    in_specs += [page_spec((1, PAGE, D_B), i) for i in range(pp)]
    in_specs += [page_spec((1, H_B, PAGE), i) for i in range(pp)]
    rows = H_B * ROWS_S
    grid_spec = pltpu.PrefetchScalarGridSpec(
        num_scalar_prefetch=1,
        grid=(n_batch, n_steps),
        in_specs=in_specs,
        out_specs=pl.BlockSpec((ROWS_S, D_B), lambda b, j, pt: (b, 0)),
        scratch_shapes=[pltpu.VMEM((rows, D_B), BF16), pltpu.VMEM((rows, 1), F32), pltpu.VMEM((rows, 1), F32),
                        pltpu.VMEM((rows, D_B), F32), pltpu.VMEM((rows, 1), F32)],
    )
    return pl.pallas_call(
        functools.partial(_fox_decode_kernel, n_new=n_new, n_steps=n_steps),
        grid_spec=grid_spec,
        out_shape=jax.ShapeDtypeStruct((n_batch * ROWS_S, D_B), BF16),
        compiler_params=_cparams(("arbitrary", "arbitrary")),
        name="fox_decode",
    )(page_table, zb, zb, zb, zb, logf_new, *([ck] * pp), *([cv] * pp), *([clf] * pp))


def _retention_tables(chunk, n_valid):
    ck = max(chunk, LANES)
    log_g = np.log1p(-np.exp2(-5.0 - np.arange(H_C, dtype=np.float64)))
    n = np.arange(chunk, dtype=np.float64)
    mk = np.arange(ck, dtype=np.float64)
    diff = n[:, None] - mk[None, :]
    dmask = np.where((diff >= 0) & (mk[None, :] < n_valid), np.exp(np.maximum(diff, 0.0)[None] * log_g[:, None, None]), 0.0)
    q_dec = np.exp((n + 1.0)[:, None] * log_g[None, :])
    k_dec = np.where(mk[:, None] < n_valid, np.exp((n_valid - 1.0 - mk)[:, None] * log_g[None, :]), 0.0)
    c_dec = np.exp(n_valid * log_g)
    expand = lambda t: np.repeat(t, DK_C, axis=1).astype(np.float32)
    return dmask.astype(np.float32), expand(q_dec), expand(k_dec), [float(x) for x in c_dec]


def _rope_tables(pos):
    half = DK_C // 2
    inv = ROPE_BASE ** (-np.arange(half, dtype=np.float64) / half)
    ang = np.asarray(pos, np.float64)[:, None] * inv[None, :]
    cos = np.concatenate([np.cos(ang), np.cos(ang)], axis=1).astype(np.float32)
    sin = np.concatenate([-np.sin(ang), np.sin(ang)], axis=1).astype(np.float32)
    return cos, sin


def _retention_kernel(q_ref, k_ref, v_ref, g_ref, cos_ref, sin_ref, dmask_ref, qdec_ref, kdec_ref, gain_ref,
                      s0_ref, o_ref, sout_ref, s_scr, *, chunk, c_dec, n_chunks):
    C = chunk
    ck = max(C, LANES)
    ci = pl.program_id(1)

    @pl.when(ci == 0)
    def _():
        s_scr[...] = s0_ref[0]

    cos = cos_ref[...]
    sin = sin_ref[...]
    rope = lambda x: x * cos + pltpu.roll(x, DK_C // 2, axis=1) * sin
    for h in range(H_C):
        ks = slice(h * DK_C, (h + 1) * DK_C)
        vs = slice(h * DV_C, (h + 1) * DV_C)
        q = rope(q_ref[:, ks]) * (DK_C ** -0.5)
        k = _pad_rows(rope(k_ref[:, ks]), ck)
        v = _pad_rows(v_ref[:, vs], ck)
        att = _dot_nt(q, k) * dmask_ref[h]
        s_h = s_scr[h]
        o = _dot(att, v) + _dot(q * qdec_ref[:, ks], s_h)
        s_scr[h] = s_h * c_dec[h] + _dot_tn(k * kdec_ref[:, ks], v)
        o = o * lax.rsqrt(jnp.mean(o * o, axis=-1, keepdims=True) + RMS_EPS) * gain_ref[:, vs]
        o_ref[:, vs] = (_silu(g_ref[:, vs]) * o).astype(o_ref.dtype)

    @pl.when(ci == n_chunks - 1)
    def _():
        sout_ref[0] = s_scr[...]


def retention(z1, n_batch, t_pad, chunk, n_valid, pos, per_batch_pos, s0, gain):
    C = chunk
    ck = max(C, LANES)
    n_chunks = t_pad // C
    dmask, q_dec, k_dec, c_dec = _retention_tables(C, n_valid)
    cos, sin = _rope_tables(pos)
    if per_batch_pos:
        pos_map = lambda b, i: (i, 0)
    else:
        pos_map = lambda b, i: (b * n_chunks + i, 0)
    rows = lambda w, blk: pl.BlockSpec((C, w), lambda b, i: (b * n_chunks + i, blk))
    cs = lambda shape: _const_spec(shape, 2)
    state_spec = pl.BlockSpec((1, H_C, DK_C, DV_C), lambda b, i: (b, 0, 0, 0))
    return pl.pallas_call(
        functools.partial(_retention_kernel, chunk=C, c_dec=c_dec, n_chunks=n_chunks),
        grid=(n_batch, n_chunks),
        in_specs=[rows(DQK_C, 0), rows(DQK_C, 1), rows(DVW_C, 1), rows(DVW_C, 2),
                  pl.BlockSpec((C, DK_C), pos_map), pl.BlockSpec((C, DK_C), pos_map),
                  cs((H_C, C, ck)), cs((C, DQK_C)), cs((ck, DQK_C)), cs((1, DVW_C)), state_spec],
        out_specs=[pl.BlockSpec((C, DVW_C), lambda b, i: (b * n_chunks + i, 0)), state_spec],
        out_shape=[jax.ShapeDtypeStruct((n_batch * t_pad, DVW_C), BF16),
                   jax.ShapeDtypeStruct((n_batch, H_C, DK_C, DV_C), F32)],
        scratch_shapes=[pltpu.VMEM((H_C, DK_C, DV_C), F32)],
        compiler_params=_cparams(("arbitrary", "arbitrary")),
        name="retention",
    )(z1, z1, z1, z1, jnp.asarray(cos), jnp.asarray(sin), jnp.asarray(dmask), jnp.asarray(q_dec),
      jnp.asarray(k_dec), gain.reshape(1, DVW_C).astype(F32), s0.astype(F32))


def _prep_weights(w_in0, w_out0, w_in1, w_out1):
    c1 = SHIFT_DIM
    c2 = c1 + D_A
    c5 = c2 + 3 * D_B
    c6 = c5 + H_B
    wc = jnp.concatenate([w_in0[:, :c1], w_in0[:, c5:c6],
                          jnp.zeros((D_MODEL, LANES - H_B), w_in0.dtype)], axis=1).astype(BF16)
    wb = jnp.concatenate([w_in0[:, c1:c5], w_in0[:, c6:]], axis=1).astype(BF16)
    return wc, wb, w_out0.astype(BF16), w_in1.astype(BF16), w_out1.astype(BF16)


def kernel(x_prompt, x_sample, state_wkv, state_shift, cache_fox_k, cache_fox_v, cache_fox_logf, state_ret,
           page_table, meta_tokens, norm0_g, w_in0, mu_shift, w0_decay, w2_decay, a0_iclr, a2_iclr, k_k, k_a,
           r_k, lnx_g, lnx_b, b_forget, w_out0, norm1_g, w_in1, ret_norm_g, w_out1, final_norm_g):
    n_b, seq, _ = x_prompt.shape
    assert n_b == 1
    n_dec, dec_seq, _ = x_sample.shape
    t_real = N_META + seq
    t_p = -(-t_real // 640) * 640 if t_real > 640 else -(-t_real // LANES) * LANES
    pad_p = t_p - t_real
    past_len = page_table.shape[1] * PAGE
    wts = dict(mu_shift=mu_shift, w0_decay=w0_decay, w2_decay=w2_decay, a0_iclr=a0_iclr, a2_iclr=a2_iclr,
               k_k=k_k, k_a=k_a, r_k=r_k, lnx_g=lnx_g, lnx_b=lnx_b)
    wc, wb, wo0, wi1, wo1 = _prep_weights(w_in0, w_out0, w_in1, w_out1)

    hp = jnp.concatenate([jnp.zeros((pad_p, D_MODEL), F32), meta_tokens.astype(F32), x_prompt[0]], axis=0)
    hs = jnp.pad(x_sample, ((0, 0), (0, ROWS_S - dec_seq), (0, 0))).reshape(n_dec * ROWS_S, D_MODEL)
    tm_p = _row_block(t_p)
    tm_s = _row_block(hs.shape[0])

    xn_p = rms_norm_rows(hp, norm0_g, BF16, tm_p)
    xn_s = rms_norm_rows(hs, norm0_g, BF16, tm_s)
    zc_p, zb_p = matmul([xn_p], wc), matmul([xn_p], wb)
    zc_s, zb_s = matmul([xn_s], wc), matmul([xn_s], wb)

    ya_p, p_wkv, p_shift = rwkv_mix(zc_p, zb_p, ZB_GATE_A, 1, t_p, t_p, jnp.zeros((1, SHIFT_DIM), F32),
                                    jnp.zeros((1, H_A, HD_A, HD_A), F32), wts)
    C = RWKV_CHUNK
    widen = lambda a, w: jnp.pad(a.reshape(n_dec, ROWS_S, -1)[:, :, :w],
                                 ((0, 0), (0, C - ROWS_S), (0, 0))).reshape(n_dec * C, w)
    ya_s, s_wkv, s_shift = rwkv_mix(widen(zc_s, SHIFT_DIM), widen(zb_s, D_A), 0, n_dec, C, dec_seq,
                                    state_shift, state_wkv, wts)
    ya_s = ya_s.reshape(n_dec, C, D_A)[:, :ROWS_S].reshape(n_dec * ROWS_S, D_A)

    logf_p, cum_p = forget_logs(zc_p, b_forget)
    logf_s, _ = forget_logs(zc_s, b_forget)
    yb_p = fox_prefill(zb_p, cum_p, pad_p, min(640, t_p))
    yb_s = fox_decode(zb_s, logf_s, page_table, cache_fox_k, cache_fox_v, cache_fox_logf, dec_seq)

    h1_p = matmul([ya_p, yb_p], wo0, res=hp)
    h1_s = matmul([ya_s, yb_s], wo0, res=hs)

    z1_p = matmul([rms_norm_rows(h1_p, norm1_g, BF16, tm_p)], wi1)
    z1_s = matmul([rms_norm_rows(h1_s, norm1_g, BF16, tm_s)], wi1)
    o_p, p_ret = retention(z1_p, 1, t_p, RET_CHUNK, RET_CHUNK, np.arange(t_p) - pad_p, False,
                           jnp.zeros((1, H_C, DK_C, DV_C), F32), ret_norm_g)
    o_s, s_ret = retention(z1_s, n_dec, ROWS_S, ROWS_S, dec_seq, past_len + np.arange(ROWS_S), True,
                           state_ret, ret_norm_g)
    h2_p = matmul([o_p], wo1, res=h1_p)
    h2_s = matmul([o_s], wo1, res=h1_s)

    lead = pad_p + N_META
    assert lead % LANES == 0
    y_prompt = rms_norm_rows(h2_p, final_norm_g, F32, LANES, drop_blocks=lead // LANES)[None]
    y_sample = rms_norm_rows(h2_s, final_norm_g, F32, tm_s).reshape(n_dec, ROWS_S, D_MODEL)[:, :dec_seq]

    heads = lambda a: a.reshape(a.shape[:-1] + (H_B, HD_B))
    p_k = heads(zb_p[pad_p:, ZB_K * D_B:(ZB_K + 1) * D_B])[None]
    p_v = heads(zb_p[pad_p:, ZB_V * D_B:(ZB_V + 1) * D_B])[None]
    p_logf = logf_p[pad_p:, :H_B][None]
    zs3 = zb_s.reshape(n_dec, ROWS_S, ZB_W)[:, :dec_seq]
    s_k = heads(zs3[:, :, ZB_K * D_B:(ZB_K + 1) * D_B])
    s_v = heads(zs3[:, :, ZB_V * D_B:(ZB_V + 1) * D_B])
    s_logf = logf_s.reshape(n_dec, ROWS_S, LANES)[:, :dec_seq, :H_B]
    return (y_prompt, y_sample, p_wkv, p_shift, p_k, p_v, p_logf, p_ret,
            s_wkv, s_shift, s_k, s_v, s_logf, s_ret)
```

```python
import functools

import numpy as np
import jax
import jax.numpy as jnp
from jax import lax
from jax.experimental import pallas as pl
from jax.experimental.pallas import tpu as pltpu

F32 = jnp.float32
BF16 = jnp.bfloat16

D_MODEL = 2048
N_META = 16
HD_A = 64
H_A = 16
D_A = 1024
LORA = 64
SHIFT_DIM = 3 * D_A + 2 * LORA
HD_B = 128
H_B = 8
D_B = 1024
H_C = 8
DK_C = 128
DV_C = 256
DQK_C = H_C * DK_C
DVW_C = H_C * DV_C
N_IN1 = 2 * DQK_C + 2 * DVW_C
PAGE = 128
RMS_EPS = 1e-6
LNX_EPS = 64e-5
ROPE_BASE = 10000.0

LANES = 128
SUBLANES = 8
VMEM_LIMIT = 56 * 1024 * 1024

ZC_W = SHIFT_DIM + LANES
ZB_W = 5 * D_B
ZB_GATE_A, ZB_Q, ZB_K, ZB_V, ZB_GATE_B = 0, 1, 2, 3, 4

RWKV_CHUNK = 64
RET_CHUNK = 128
ROWS_S = 16
PAGES_PER_STEP = 4


def _cparams(sem):
    return pltpu.CompilerParams(dimension_semantics=sem, vmem_limit_bytes=VMEM_LIMIT)


def _dot(a, b):
    return jnp.dot(a.astype(BF16), b.astype(BF16), preferred_element_type=F32)


def _dot_nt(a, b):
    return lax.dot_general(a.astype(BF16), b.astype(BF16), (((1,), (1,)), ((), ())),
                           preferred_element_type=F32)


def _dot_tn(a, b):
    return jnp.dot(a.T.astype(BF16), b.astype(BF16), preferred_element_type=F32)


def _split3(x):
    hi = x.astype(BF16)
    r1 = x - hi.astype(F32)
    mid = r1.astype(BF16)
    lo = (r1 - mid.astype(F32)).astype(BF16)
    return hi, mid, lo


def _dot_sel(x, sel):
    hi, mid, lo = _split3(x)
    f = lambda p: jnp.dot(p, sel, preferred_element_type=F32)
    return f(hi) + f(mid) + f(lo)


def _sel_dot(sel, x):
    hi, mid, lo = _split3(x)
    f = lambda p: jnp.dot(sel, p, preferred_element_type=F32)
    return f(hi) + f(mid) + f(lo)


def _sel_dot_nt(sel, x):
    hi, mid, lo = _split3(x)
    f = lambda p: lax.dot_general(sel, p, (((1,), (1,)), ((), ())), preferred_element_type=F32)
    return f(hi) + f(mid) + f(lo)


def _mask01(m):
    return jnp.where(m, 1.0, 0.0).astype(BF16)


def _sigmoid(x):
    return 1.0 / (1.0 + jnp.exp(-x))


def _silu(x):
    return x * _sigmoid(x)


def _softplus(x):
    return jnp.maximum(x, 0.0) + jnp.log(1.0 + jnp.exp(-jnp.abs(x)))


def _pad_rows(x, rows):
    if x.shape[0] == rows:
        return x
    return jnp.concatenate([x, jnp.zeros((rows - x.shape[0], x.shape[1]), x.dtype)], axis=0)


def _row_block(m):
    for t in (1664, 1024, 640, 512, 256, 128, 64, 32, 16):
        if m % t == 0:
            return t
    raise ValueError(m)


def _const_spec(shape, n_grid):
    zeros = tuple(0 for _ in shape)
    if n_grid == 1:
        return pl.BlockSpec(shape, lambda i: zeros)
    return pl.BlockSpec(shape, lambda i, j: zeros)


def _norm_kernel(x_ref, g_ref, o_ref):
    x = x_ref[...]
    y = x * lax.rsqrt(jnp.mean(x * x, axis=-1, keepdims=True) + RMS_EPS)
    o_ref[...] = (y * g_ref[...]).astype(o_ref.dtype)


def rms_norm_rows(x, g, out_dtype, tm, drop_blocks=0):
    m, d = x.shape
    return pl.pallas_call(
        _norm_kernel,
        grid=(m // tm,),
        in_specs=[pl.BlockSpec((tm, d), lambda i: (i, 0)), _const_spec((1, d), 1)],
        out_specs=pl.BlockSpec((tm, d), lambda i: (jnp.maximum(i - drop_blocks, 0), 0)),
        out_shape=jax.ShapeDtypeStruct((m - drop_blocks * tm, d), out_dtype),
        compiler_params=_cparams(("arbitrary",)),
        name="rms_norm",
    )(x, g.reshape(1, d).astype(F32))


def _mm_kernel(x_ref, w_ref, o_ref):
    o_ref[...] = jnp.dot(x_ref[...], w_ref[...], preferred_element_type=F32)


def _mm_res_kernel(x_ref, w_ref, r_ref, o_ref):
    o_ref[...] = r_ref[...] + jnp.dot(x_ref[...], w_ref[...], preferred_element_type=F32)


def _mm2_res_kernel(xa_ref, xb_ref, w_ref, r_ref, o_ref):
    ka = xa_ref.shape[1]
    acc = jnp.dot(xa_ref[...], w_ref[:ka, :], preferred_element_type=F32)
    acc += jnp.dot(xb_ref[...], w_ref[ka:, :], preferred_element_type=F32)
    o_ref[...] = r_ref[...] + acc


def matmul(xs, w, res=None):
    m = xs[0].shape[0]
    k, n = w.shape
    tm = _row_block(m)
    tn = 512 if n % 512 == 0 else 256
    assert n % tn == 0 and sum(x.shape[1] for x in xs) == k
    in_specs = [pl.BlockSpec((tm, x.shape[1]), lambda i, j: (i, 0)) for x in xs]
    in_specs.append(pl.BlockSpec((k, tn), lambda i, j: (0, j)))
    args = list(xs) + [w]
    if res is not None:
        in_specs.append(pl.BlockSpec((tm, tn), lambda i, j: (i, j)))
        args.append(res)
        body = _mm_res_kernel if len(xs) == 1 else _mm2_res_kernel
    else:
        assert len(xs) == 1
        body = _mm_kernel
    return pl.pallas_call(
        body,
        grid=(m // tm, n // tn),
        in_specs=in_specs,
        out_specs=pl.BlockSpec((tm, tn), lambda i, j: (i, j)),
        out_shape=jax.ShapeDtypeStruct((m, n), F32),
        compiler_params=_cparams(("arbitrary", "arbitrary")),
        name="matmul",
    )(*args)


def _rwkv_kernel(zc_ref, ga_ref, shift0_ref, s0_ref, mu_ref, w0_ref, a0_ref, lora_ref, kk_ref, ka_ref,
                 rk_ref, lng_ref, lnb_ref, seg_ref, segt_ref,
                 y_ref, sout_ref, shiftout_ref,
                 s_scr, carry_scr, *, chunk, t_valid, n_chunks):
    C = chunk
    log2c = C.bit_length() - 1
    assert C == 1 << log2c
    ci = pl.program_id(1)

    @pl.when(ci == 0)
    def _():
        s_scr[...] = s0_ref[0]
        carry_scr[...] = shift0_ref[0]

    c = zc_ref[...]
    row = lax.broadcasted_iota(jnp.int32, (C, 1), 0)
    prev = jnp.where(row == 0, carry_scr[...], pltpu.roll(c, 1, axis=0))
    carry_scr[...] = c[C - 1:C, :]
    xs = c + (prev - c) * mu_ref[...]
    r = xs[:, :D_A]
    k = xs[:, D_A:2 * D_A]
    v = xs[:, 2 * D_A:3 * D_A]
    lo = xs[:, 3 * D_A:]
    lane = lax.broadcasted_iota(jnp.int32, (1, LANES), 1)
    lo = jnp.where(lane < LORA, jnp.tanh(lo), lo)
    proj = _dot(lo, lora_ref[...])
    w_log = -_softplus(-(w0_ref[...] + proj[:, :D_A])) - 0.5
    ell = -jnp.exp(w_log)
    asig = _sigmoid(a0_ref[...] + proj[:, D_A:])

    seg = seg_ref[...]
    segt = segt_ref[...]
    head_sum = lambda t: _dot_sel(_dot_sel(t, seg), segt)

    kkv = k * kk_ref[...]
    kkv = kkv * lax.rsqrt(head_sum(kkv * kkv) + 1e-12)
    kmod = k * (1.0 + (asig - 1.0) * ka_ref[...])
    a_vec = -kkv
    b_vec = kkv * asig
    if t_valid % C:
        ok = (ci * C + row) < t_valid
        ell = jnp.where(ok, ell, 0.0)
        a_vec = jnp.where(ok, a_vec, 0.0)
        b_vec = jnp.where(ok, b_vec, 0.0)
        kmod = jnp.where(ok, kmod, 0.0)
        v = jnp.where(ok, v, 0.0)

    ti = lax.broadcasted_iota(jnp.int32, (C, C), 0)
    si = lax.broadcasted_iota(jnp.int32, (C, C), 1)
    cum = _sel_dot(_mask01(si <= ti), ell)
    e_in = jnp.exp(cum)
    e_ex = jnp.exp(cum - ell)
    e_inv = jnp.exp(-cum)
    g_end = e_in[C - 1:C, :]
    a_t = a_vec * e_ex
    r_t = r * e_in
    k_t = kmod * e_inv
    b_t = b_vec * e_inv
    k_h = k_t * g_end
    b_h = b_t * g_end

    first = lane < HD_A
    t2 = lax.broadcasted_iota(jnp.int32, (2 * C, 2 * C), 0)
    s2 = lax.broadcasted_iota(jnp.int32, (2 * C, 2 * C), 1)
    same = (t2 >> log2c) == (s2 >> log2c)
    tin = t2 & (C - 1)
    sin = s2 & (C - 1)
    strict = jnp.logical_and(same, sin < tin)
    incl = jnp.logical_and(same, sin <= tin)
    eye = jnp.where(t2 == s2, 1.0, 0.0)

    def stack(x, g):
        xg = x[:, g * LANES:(g + 1) * LANES]
        return jnp.concatenate([jnp.where(first, xg, 0.0), jnp.where(first, 0.0, xg)], axis=0)

    gs = range(D_A // LANES)
    cat0 = lambda a, b: jnp.concatenate([a, b], axis=0)
    cat1 = lambda a, b: jnp.concatenate([a, b], axis=1)
    C2 = 2 * C
    ast = [stack(a_t, g).astype(BF16) for g in gs]
    rst = [stack(r_t, g).astype(BF16) for g in gs]
    vst = [stack(v, g) for g in gs]
    vst_b = [x.astype(BF16) for x in vst]
    bk_t = [cat0(stack(b_t, g), stack(k_t, g)).astype(BF16) for g in gs]
    bk_h = [cat0(stack(b_h, g), stack(k_h, g)).astype(BF16) for g in gs]
    prod = [_dot_nt(cat0(ast[g], rst[g]), bk_t[g]) for g in gs]
    x_ab = [jnp.where(strict, p[:C2, :C2], 0.0) for p in prod]
    x_ak = [jnp.where(strict, p[:C2, C2:], 0.0) for p in prod]
    x_r = [cat1(jnp.where(incl, p[C2:, :C2], 0.0), jnp.where(incl, p[C2:, C2:], 0.0)).astype(BF16) for p in prod]
    tinv = [eye + x for x in x_ab]
    xp = [_dot(x, x) for x in x_ab]
    n = 2
    while n < C:
        if 2 * n < C:
            both = [_dot(cat0(xp[g], tinv[g]), xp[g]) for g in gs]
            xp = [b[:C2] for b in both]
            tinv = [tinv[g] + both[g][C2:] for g in gs]
        else:
            tinv = [tinv[g] + _dot(tinv[g], xp[g]) for g in gs]
        n *= 2
    akv = [_dot(x_ak[g], vst_b[g]) for g in gs]
    ta = [_dot(tinv[g], cat1(ast[g], akv[g].astype(BF16))) for g in gs]
    s_old = [s_scr[g] for g in gs]
    ar = [_dot_nt(cat0(ta[g][:, :LANES].astype(BF16), rst[g]), s_old[g]) for g in gs]
    u = [ar[g][:C2] + ta[g][:, LANES:] for g in gs]
    uv = [cat0(u[g], vst[g]) for g in gs]
    yst = [ar[g][C2:] + _dot(x_r[g], uv[g]) for g in gs]
    for g in gs:
        s_scr[g] = s_old[g] * g_end[:, g * LANES:(g + 1) * LANES] + _dot_tn(uv[g], bk_h[g])

    y = jnp.concatenate([t[:C] + t[C:] for t in yst], axis=1)
    inv_hd = 1.0 / HD_A
    mean = head_sum(y) * inv_hd
    d = y - mean
    var = head_sum(d * d) * inv_hd
    y = d * lax.rsqrt(var + LNX_EPS) * lng_ref[...] + lnb_ref[...]
    y = y + head_sum(r * kmod * rk_ref[...]) * v
    y_ref[...] = (y * _silu(ga_ref[...])).astype(y_ref.dtype)

    @pl.when(ci == n_chunks - 1)
    def _():
        sout_ref[0] = s_scr[...]
        last = (t_valid - 1) % C
        shiftout_ref[0] = c[last:last + 1, :]


def _pair_state(s):
    b = s.shape[0]
    s = s.reshape(b, H_A // 2, 2, HD_A, HD_A)
    z = jnp.zeros_like(s[:, :, 0])
    top = jnp.concatenate([s[:, :, 0], z], axis=-1)
    bot = jnp.concatenate([z, s[:, :, 1]], axis=-1)
    return jnp.concatenate([top, bot], axis=-2)


def _unpair_state(s):
    b = s.shape[0]
    s = s.reshape(b, H_A // 2, 2, HD_A, 2, HD_A)
    return jnp.stack([s[:, :, 0, :, 0, :], s[:, :, 1, :, 1, :]], axis=2).reshape(b, H_A, HD_A, HD_A)


def rwkv_mix(zc, gate_a_src, gate_blk, n_batch, t_pad, t_valid, shift0, s0, wts):
    C = RWKV_CHUNK
    assert t_pad % C == 0
    n_chunks = t_pad // C
    row = lambda a: a.reshape(1, -1).astype(F32)
    lora = jnp.zeros((LANES, 2 * D_A), F32)
    lora = lora.at[:LORA, :D_A].set(wts["w2_decay"]).at[LORA:, D_A:].set(wts["a2_iclr"])
    head_of = np.arange(D_A) // HD_A
    seg = (head_of[:, None] == np.arange(LANES)[None, :]).astype(np.float32)
    cs = lambda shape: _const_spec(shape, 2)
    kern = functools.partial(_rwkv_kernel, chunk=C, t_valid=t_valid, n_chunks=n_chunks)
    y, s_out, shift_out = pl.pallas_call(
        kern,
        grid=(n_batch, n_chunks),
        in_specs=[
            pl.BlockSpec((C, SHIFT_DIM), lambda b, i: (b * n_chunks + i, 0)),
            pl.BlockSpec((C, D_A), lambda b, i: (b * n_chunks + i, gate_blk)),
            pl.BlockSpec((1, 1, SHIFT_DIM), lambda b, i: (b, 0, 0)),
            pl.BlockSpec((1, H_A // 2, LANES, LANES), lambda b, i: (b, 0, 0, 0)),
            cs((1, SHIFT_DIM)), cs((1, D_A)), cs((1, D_A)), cs((LANES, 2 * D_A)),
            cs((1, D_A)), cs((1, D_A)), cs((1, D_A)), cs((1, D_A)), cs((1, D_A)),
            cs((D_A, LANES)), cs((LANES, D_A)),
        ],
        out_specs=[
            pl.BlockSpec((C, D_A), lambda b, i: (b * n_chunks + i, 0)),
            pl.BlockSpec((1, H_A // 2, LANES, LANES), lambda b, i: (b, 0, 0, 0)),
            pl.BlockSpec((1, 1, SHIFT_DIM), lambda b, i: (b, 0, 0)),
        ],
        out_shape=[
            jax.ShapeDtypeStruct((n_batch * t_pad, D_A), BF16),
            jax.ShapeDtypeStruct((n_batch, H_A // 2, LANES, LANES), F32),
            jax.ShapeDtypeStruct((n_batch, 1, SHIFT_DIM), F32),
        ],
        scratch_shapes=[pltpu.VMEM((H_A // 2, LANES, LANES), F32), pltpu.VMEM((1, SHIFT_DIM), F32)],
        compiler_params=_cparams(("arbitrary", "arbitrary")),
        name="rwkv7_mix",
    )(zc, gate_a_src, shift0.reshape(n_batch, 1, SHIFT_DIM).astype(F32), _pair_state(s0.astype(F32)),
      row(wts["mu_shift"]), row(wts["w0_decay"]), row(wts["a0_iclr"]), lora,
      row(wts["k_k"]), row(wts["k_a"]), row(wts["r_k"]), row(wts["lnx_g"]), row(wts["lnx_b"]),
      jnp.asarray(seg, BF16), jnp.asarray(seg.T, BF16))
    return y, _unpair_state(s_out), shift_out.reshape(n_batch, SHIFT_DIM)


NO_KEY = 1e30


def _logf_kernel(f_ref, bias_ref, logf_ref, cum_t_ref, carry_scr, *, tb, pad):
    i = pl.program_id(0)

    @pl.when(i == 0)
    def _():
        carry_scr[...] = jnp.zeros_like(carry_scr)

    x = f_ref[...] + bias_ref[...]
    logf = jnp.minimum(x, 0.0) - jnp.log(1.0 + jnp.exp(-jnp.abs(x)))
    logf_ref[...] = logf
    ti = lax.broadcasted_iota(jnp.int32, (tb, tb), 0)
    si = lax.broadcasted_iota(jnp.int32, (tb, tb), 1)
    cum = _sel_dot(_mask01(si <= ti), logf) + carry_scr[...]
    carry_scr[...] = cum[tb - 1:tb, :]
    pos = i * tb + lax.broadcasted_iota(jnp.int32, (SUBLANES, tb), 1)
    cum_t_ref[...] = jnp.where(pos < pad, NO_KEY, cum.T[:SUBLANES, :])


def forget_logs(zc, b_forget, pad=0):
    m = zc.shape[0]
    tb = LANES
    bias = jnp.zeros((1, LANES), F32).at[0, :H_B].set(b_forget.astype(F32))
    return pl.pallas_call(
        functools.partial(_logf_kernel, tb=tb, pad=pad),
        grid=(m // tb,),
        in_specs=[pl.BlockSpec((tb, LANES), lambda i: (i, SHIFT_DIM // LANES)), _const_spec((1, LANES), 1)],
        out_specs=[pl.BlockSpec((tb, LANES), lambda i: (i, 0)),
                   pl.BlockSpec((SUBLANES, tb), lambda i: (0, i))],
        out_shape=[jax.ShapeDtypeStruct((m, LANES), F32), jax.ShapeDtypeStruct((SUBLANES, m), F32)],
        scratch_shapes=[pltpu.VMEM((1, LANES), F32)],
        compiler_params=_cparams(("arbitrary",)),
        name="forget_logs",
    )(zc, bias)


LOG2E = 1.4426950408889634


def _fox_prefill_kernel(q_ref, k_ref, v_ref, gate_ref, cum_ref, o_ref, q_scr, m_scr, l_scr, acc_scr, *, tq):
    qi = pl.program_id(0)
    ki = pl.program_id(1)

    @pl.when(ki == 0)
    def _():
        q_scr[...] = (q_ref[...] * (HD_B ** -0.5 * LOG2E)).astype(BF16)
        m_scr[...] = jnp.full_like(m_scr, -jnp.inf)
        l_scr[...] = jnp.zeros_like(l_scr)
        acc_scr[...] = jnp.zeros_like(acc_scr)

    def block(causal):
        if causal:
            keep = lax.broadcasted_iota(jnp.int32, (tq, tq), 1) <= lax.broadcasted_iota(jnp.int32, (tq, tq), 0)
        for h in range(H_B):
            sl = slice(h * HD_B, (h + 1) * HD_B)
            s = _dot_nt(q_scr[:, sl], k_ref[:, sl]) - cum_ref[h:h + 1, :] * LOG2E
            if causal:
                s = jnp.where(keep, s, -jnp.inf)
            m_old = m_scr[h]
            m_new = jnp.maximum(m_old, jnp.max(s, axis=-1, keepdims=True))
            alpha = jnp.exp2(m_old - m_new)
            p = jnp.exp2(s - m_new)
            l_scr[h] = alpha * l_scr[h] + jnp.sum(p, axis=-1, keepdims=True)
            acc_scr[:, sl] = alpha * acc_scr[:, sl] + _dot(p, v_ref[:, sl])
            m_scr[h] = m_new

    @pl.when(ki < qi)
    def _():
        block(False)

    @pl.when(ki == qi)
    def _():
        block(True)
        for h in range(H_B):
            sl = slice(h * HD_B, (h + 1) * HD_B)
            y = acc_scr[:, sl] / l_scr[h]
            o_ref[:, sl] = (y * _silu(gate_ref[:, sl])).astype(o_ref.dtype)


def fox_prefill(zb, cum_t, tq):
    m = zb.shape[0]
    nq = m // tq
    kv = lambda blk: pl.BlockSpec((tq, D_B), lambda qi, ki: (jnp.minimum(ki, qi), blk))
    return pl.pallas_call(
        functools.partial(_fox_prefill_kernel, tq=tq),
        grid=(nq, nq),
        in_specs=[
            pl.BlockSpec((tq, D_B), lambda qi, ki: (qi, ZB_Q)),
            kv(ZB_K), kv(ZB_V),
            pl.BlockSpec((tq, D_B), lambda qi, ki: (qi, ZB_GATE_B)),
            pl.BlockSpec((SUBLANES, tq), lambda qi, ki: (0, jnp.minimum(ki, qi))),
        ],
        out_specs=pl.BlockSpec((tq, D_B), lambda qi, ki: (qi, 0)),
        out_shape=jax.ShapeDtypeStruct((m, D_B), BF16),
        scratch_shapes=[pltpu.VMEM((tq, D_B), BF16), pltpu.VMEM((H_B, tq, 1), F32),
                        pltpu.VMEM((H_B, tq, 1), F32), pltpu.VMEM((tq, D_B), F32)],
        compiler_params=_cparams(("arbitrary", "arbitrary")),
        name="fox_prefill",
    )(zb, zb, zb, zb, cum_t)


def _fox_decode_kernel(pt_ref, q_ref, kn_ref, vn_ref, gate_ref, lfn_ref, *rest, n_new, n_steps):
    pp = PAGES_PER_STEP
    k_refs, v_refs, lf_refs = rest[:pp], rest[pp:2 * pp], rest[2 * pp:3 * pp]
    o_ref = rest[3 * pp]
    q_scr, m_scr, l_scr, acc_scr, off_scr = rest[3 * pp + 1:]
    del pt_ref
    j = pl.program_id(1)
    R = ROWS_S
    rows = H_B * R
    heads = range(H_B)
    hsl = lambda h: slice(h * HD_B, (h + 1) * HD_B)

    def online(s, values):
        m_old = m_scr[...]
        m_new = jnp.maximum(m_old, jnp.max(s, axis=-1, keepdims=True))
        alpha = jnp.exp(m_old - m_new)
        p = jnp.exp(s - m_new)
        l_scr[...] = alpha * l_scr[...] + jnp.sum(p, axis=-1, keepdims=True)
        pb = p.astype(BF16)
        pv = jnp.concatenate([jnp.dot(pb[h * R:(h + 1) * R], values(h), preferred_element_type=F32)
                              for h in heads], axis=0)
        acc_scr[...] = alpha * acc_scr[...] + pv
        m_scr[...] = m_new

    @pl.when(j == 0)
    def _():
        q = q_ref[...] * (HD_B ** -0.5)
        for h in heads:
            q_scr[h] = q[:, hsl(h)].astype(BF16)
        m_scr[...] = jnp.full_like(m_scr, -jnp.inf)
        l_scr[...] = jnp.zeros_like(l_scr)
        acc_scr[...] = jnp.zeros_like(acc_scr)
        off_scr[...] = jnp.zeros_like(off_scr)
        lf = lfn_ref[...]
        trow = lax.broadcasted_iota(jnp.int32, (R, 1), 0)
        cn = lf
        sh = 1
        while sh < R:
            cn = cn + jnp.where(trow >= sh, pltpu.roll(cn, sh, axis=0), 0.0)
            sh *= 2
        r_i = lax.broadcasted_iota(jnp.int32, (rows, LANES), 0)
        c_i = lax.broadcasted_iota(jnp.int32, (rows, LANES), 1)
        pick = _mask01(c_i == (r_i >> (R.bit_length() - 1)))
        c_keys = _sel_dot_nt(pick, _pad_rows(cn, LANES))
        kn = kn_ref[...]
        vn = vn_ref[...]
        s = jnp.concatenate([_dot_nt(q_scr[h], _pad_rows(kn[:, hsl(h)], LANES)) for h in heads], axis=0)
        s = jnp.where(jnp.logical_and(c_i <= (r_i & (R - 1)), c_i < n_new), s - c_keys, -jnp.inf)
        online(s, lambda h: _pad_rows(vn[:, hsl(h)], LANES).astype(BF16))

    k_i = lax.broadcasted_iota(jnp.int32, (PAGE, PAGE), 0)
    k_j = lax.broadcasted_iota(jnp.int32, (PAGE, PAGE), 1)
    later = _mask01(k_i > k_j)
    off = off_scr[...]
    bias = []
    for i in range(pp):
        lf_t = lf_refs[i][0]
        lf_rows = jnp.concatenate([jnp.broadcast_to(lf_t[h:h + 1, :], (R, PAGE)) for h in heads], axis=0)
        bias.append(_dot_sel(lf_rows, later) + off)
        off = off + jnp.sum(lf_rows, axis=-1, keepdims=True)
    off_scr[...] = off
    head_rows = lambda ref, h: ref[0, pl.ds(h, PAGE, stride=H_B), :]
    keys = lambda h: jnp.concatenate([head_rows(k_refs[i], h).astype(BF16) for i in range(pp)], axis=0)
    s = jnp.concatenate([_dot_nt(q_scr[h], keys(h)) for h in heads], axis=0)
    s = s + jnp.concatenate(bias, axis=1)
    online(s, lambda h: jnp.concatenate([head_rows(v_refs[i], h).astype(BF16) for i in range(pp)], axis=0))

    @pl.when(j == n_steps - 1)
    def _():
        y = acc_scr[...] / l_scr[...]
        y = jnp.concatenate([y[h * R:(h + 1) * R] for h in heads], axis=1)
        o_ref[...] = (y * _silu(gate_ref[...])).astype(o_ref.dtype)


def fox_decode(zb, logf_new, page_table, cache_k, cache_v, cache_logf, n_new):
    n_batch, n_pages = page_table.shape
    n_pool = cache_k.shape[0]
    pp = PAGES_PER_STEP
    assert n_pages % pp == 0
    n_steps = n_pages // pp
    ck = cache_k.reshape(n_pool, PAGE * H_B, HD_B)
    cv = cache_v.reshape(n_pool, PAGE * H_B, HD_B)
    clf = jnp.swapaxes(cache_logf, 1, 2)
    rowblk = lambda blk: pl.BlockSpec((ROWS_S, D_B), lambda b, j, pt: (b, blk))

    def page_spec(shape, i):
        return pl.BlockSpec(shape, lambda b, j, pt: (pt[b, n_pages - 1 - (j * pp + i)], 0, 0))

    in_specs = [rowblk(ZB_Q), rowblk(ZB_K), rowblk(ZB_V), rowblk(ZB_GATE_B),
                pl.BlockSpec((ROWS_S, LANES), lambda b, j, pt: (b, 0))]
    in_specs += [page_spec((1, PAGE * H_B, HD_B), i) for i in range(pp)]
    in_specs += [page_spec((1, PAGE * H_B, HD_B), i) for i in range(pp)]
    in_specs += [page_spec((1, H_B, PAGE), i) for i in range(pp)]
    rows = H_B * ROWS_S
    grid_spec = pltpu.PrefetchScalarGridSpec(
        num_scalar_prefetch=1,
        grid=(n_batch, n_steps),
        in_specs=in_specs,
        out_specs=pl.BlockSpec((ROWS_S, D_B), lambda b, j, pt: (b, 0)),
        scratch_shapes=[pltpu.VMEM((H_B, ROWS_S, HD_B), BF16), pltpu.VMEM((rows, 1), F32),
                        pltpu.VMEM((rows, 1), F32), pltpu.VMEM((rows, HD_B), F32), pltpu.VMEM((rows, 1), F32)],
    )
    return pl.pallas_call(
        functools.partial(_fox_decode_kernel, n_new=n_new, n_steps=n_steps),
        grid_spec=grid_spec,
        out_shape=jax.ShapeDtypeStruct((n_batch * ROWS_S, D_B), BF16),
        compiler_params=_cparams(("arbitrary", "arbitrary")),
        name="fox_decode",
    )(page_table, zb, zb, zb, zb, logf_new, *([ck] * pp), *([cv] * pp), *([clf] * pp))


def _retention_tables(chunk, n_valid):
    ck = max(chunk, LANES)
    log_g = np.log1p(-np.exp2(-5.0 - np.arange(H_C, dtype=np.float64)))
    n = np.arange(chunk, dtype=np.float64)
    mk = np.arange(ck, dtype=np.float64)
    diff = n[:, None] - mk[None, :]
    dmask = np.where((diff >= 0) & (mk[None, :] < n_valid), np.exp(np.maximum(diff, 0.0)[None] * log_g[:, None, None]), 0.0)
    q_dec = np.exp((n + 1.0)[:, None] * log_g[None, :])
    k_dec = np.where(mk[:, None] < n_valid, np.exp((n_valid - 1.0 - mk)[:, None] * log_g[None, :]), 0.0)
    c_dec = np.exp(n_valid * log_g)
    expand = lambda t: np.repeat(t, DK_C, axis=1).astype(np.float32)
    return dmask.astype(np.float32), expand(q_dec), expand(k_dec), [float(x) for x in c_dec]


def _rope_tables(pos):
    half = DK_C // 2
    inv = ROPE_BASE ** (-np.arange(half, dtype=np.float64) / half)
    ang = np.asarray(pos, np.float64)[:, None] * inv[None, :]
    cos = np.concatenate([np.cos(ang), np.cos(ang)], axis=1).astype(np.float32)
    sin = np.concatenate([-np.sin(ang), np.sin(ang)], axis=1).astype(np.float32)
    return cos, sin


def _retention_kernel(q_ref, k_ref, v_ref, g_ref, cos_ref, sin_ref, dmask_ref, qdec_ref, kdec_ref, gain_ref,
                      s0_ref, o_ref, sout_ref, s_scr, *, chunk, c_dec, n_chunks):
    C = chunk
    ck = max(C, LANES)
    ci = pl.program_id(1)

    @pl.when(ci == 0)
    def _():
        s_scr[...] = s0_ref[0]

    cos = cos_ref[...]
    sin = sin_ref[...]
    rope = lambda x: x * cos + pltpu.roll(x, DK_C // 2, axis=1) * sin
    for h in range(H_C):
        ks = slice(h * DK_C, (h + 1) * DK_C)
        vs = slice(h * DV_C, (h + 1) * DV_C)
        q = rope(q_ref[:, ks]) * (DK_C ** -0.5)
        k = _pad_rows(rope(k_ref[:, ks]), ck)
        v = _pad_rows(v_ref[:, vs], ck)
        att = _dot_nt(q, k) * dmask_ref[h]
        s_h = s_scr[h]
        o = _dot(att, v) + _dot(q * qdec_ref[:, ks], s_h)
        s_scr[h] = s_h * c_dec[h] + _dot_tn(k * kdec_ref[:, ks], v)
        o = o * lax.rsqrt(jnp.mean(o * o, axis=-1, keepdims=True) + RMS_EPS) * gain_ref[:, vs]
        o_ref[:, vs] = (_silu(g_ref[:, vs]) * o).astype(o_ref.dtype)

    @pl.when(ci == n_chunks - 1)
    def _():
        sout_ref[0] = s_scr[...]


def retention(z1, n_batch, t_pad, chunk, n_valid, pos, per_batch_pos, s0, gain):
    C = chunk
    ck = max(C, LANES)
    n_chunks = t_pad // C
    dmask, q_dec, k_dec, c_dec = _retention_tables(C, n_valid)
    cos, sin = _rope_tables(pos)
    if per_batch_pos:
        pos_map = lambda b, i: (i, 0)
    else:
        pos_map = lambda b, i: (b * n_chunks + i, 0)
    rows = lambda w, blk: pl.BlockSpec((C, w), lambda b, i: (b * n_chunks + i, blk))
    cs = lambda shape: _const_spec(shape, 2)
    state_spec = pl.BlockSpec((1, H_C, DK_C, DV_C), lambda b, i: (b, 0, 0, 0))
    return pl.pallas_call(
        functools.partial(_retention_kernel, chunk=C, c_dec=c_dec, n_chunks=n_chunks),
        grid=(n_batch, n_chunks),
        in_specs=[rows(DQK_C, 0), rows(DQK_C, 1), rows(DVW_C, 1), rows(DVW_C, 2),
                  pl.BlockSpec((C, DK_C), pos_map), pl.BlockSpec((C, DK_C), pos_map),
                  cs((H_C, C, ck)), cs((C, DQK_C)), cs((ck, DQK_C)), cs((1, DVW_C)), state_spec],
        out_specs=[pl.BlockSpec((C, DVW_C), lambda b, i: (b * n_chunks + i, 0)), state_spec],
        out_shape=[jax.ShapeDtypeStruct((n_batch * t_pad, DVW_C), BF16),
                   jax.ShapeDtypeStruct((n_batch, H_C, DK_C, DV_C), F32)],
        scratch_shapes=[pltpu.VMEM((H_C, DK_C, DV_C), F32)],
        compiler_params=_cparams(("arbitrary", "arbitrary")),
        name="retention",
    )(z1, z1, z1, z1, jnp.asarray(cos), jnp.asarray(sin), jnp.asarray(dmask), jnp.asarray(q_dec),
      jnp.asarray(k_dec), gain.reshape(1, DVW_C).astype(F32), s0.astype(F32))


def _prep_weights(w_in0, w_out0, w_in1, w_out1):
    c1 = SHIFT_DIM
    c2 = c1 + D_A
    c5 = c2 + 3 * D_B
    c6 = c5 + H_B
    wc = jnp.concatenate([w_in0[:, :c1], w_in0[:, c5:c6],
                          jnp.zeros((D_MODEL, LANES - H_B), w_in0.dtype)], axis=1).astype(BF16)
    wb = jnp.concatenate([w_in0[:, c1:c5], w_in0[:, c6:]], axis=1).astype(BF16)
    return wc, wb, w_out0.astype(BF16), w_in1.astype(BF16), w_out1.astype(BF16)


def kernel(x_prompt, x_sample, state_wkv, state_shift, cache_fox_k, cache_fox_v, cache_fox_logf, state_ret,
           page_table, meta_tokens, norm0_g, w_in0, mu_shift, w0_decay, w2_decay, a0_iclr, a2_iclr, k_k, k_a,
           r_k, lnx_g, lnx_b, b_forget, w_out0, norm1_g, w_in1, ret_norm_g, w_out1, final_norm_g):
    n_b, seq, _ = x_prompt.shape
    assert n_b == 1
    n_dec, dec_seq, _ = x_sample.shape
    t_real = N_META + seq
    t_p = -(-t_real // 640) * 640 if t_real > 640 else -(-t_real // LANES) * LANES
    pad_p = t_p - t_real
    past_len = page_table.shape[1] * PAGE
    wts = dict(mu_shift=mu_shift, w0_decay=w0_decay, w2_decay=w2_decay, a0_iclr=a0_iclr, a2_iclr=a2_iclr,
               k_k=k_k, k_a=k_a, r_k=r_k, lnx_g=lnx_g, lnx_b=lnx_b)
    wc, wb, wo0, wi1, wo1 = _prep_weights(w_in0, w_out0, w_in1, w_out1)

    hp = jnp.concatenate([jnp.zeros((pad_p, D_MODEL), F32), meta_tokens.astype(F32), x_prompt[0]], axis=0)
    hs = jnp.pad(x_sample, ((0, 0), (0, ROWS_S - dec_seq), (0, 0))).reshape(n_dec * ROWS_S, D_MODEL)
    tm_p = _row_block(t_p)
    tm_s = _row_block(hs.shape[0])

    xn_p = rms_norm_rows(hp, norm0_g, BF16, tm_p)
    xn_s = rms_norm_rows(hs, norm0_g, BF16, tm_s)
    zc_p, zb_p = matmul([xn_p], wc), matmul([xn_p], wb)
    zc_s, zb_s = matmul([xn_s], wc), matmul([xn_s], wb)

    ya_p, p_wkv, p_shift = rwkv_mix(zc_p, zb_p, ZB_GATE_A, 1, t_p, t_p, jnp.zeros((1, SHIFT_DIM), F32),
                                    jnp.zeros((1, H_A, HD_A, HD_A), F32), wts)
    C = RWKV_CHUNK
    widen = lambda a, w: jnp.pad(a.reshape(n_dec, ROWS_S, -1)[:, :, :w],
                                 ((0, 0), (0, C - ROWS_S), (0, 0))).reshape(n_dec * C, w)
    ya_s, s_wkv, s_shift = rwkv_mix(widen(zc_s, SHIFT_DIM), widen(zb_s, D_A), 0, n_dec, C, dec_seq,
                                    state_shift, state_wkv, wts)
    ya_s = ya_s.reshape(n_dec, C, D_A)[:, :ROWS_S].reshape(n_dec * ROWS_S, D_A)

    logf_p, cum_p = forget_logs(zc_p, b_forget, pad_p)
    logf_s, _ = forget_logs(zc_s, b_forget)
    yb_p = fox_prefill(zb_p, cum_p, min(640, t_p))
    yb_s = fox_decode(zb_s, logf_s, page_table, cache_fox_k, cache_fox_v, cache_fox_logf, dec_seq)

    h1_p = matmul([ya_p, yb_p], wo0, res=hp)
    h1_s = matmul([ya_s, yb_s], wo0, res=hs)

    z1_p = matmul([rms_norm_rows(h1_p, norm1_g, BF16, tm_p)], wi1)
    z1_s = matmul([rms_norm_rows(h1_s, norm1_g, BF16, tm_s)], wi1)
    o_p, p_ret = retention(z1_p, 1, t_p, RET_CHUNK, RET_CHUNK, np.arange(t_p) - pad_p, False,
                           jnp.zeros((1, H_C, DK_C, DV_C), F32), ret_norm_g)
    o_s, s_ret = retention(z1_s, n_dec, ROWS_S, ROWS_S, dec_seq, past_len + np.arange(ROWS_S), True,
                           state_ret, ret_norm_g)
    h2_p = matmul([o_p], wo1, res=h1_p)
    h2_s = matmul([o_s], wo1, res=h1_s)

    lead = pad_p + N_META
    assert lead % LANES == 0
    y_prompt = rms_norm_rows(h2_p, final_norm_g, F32, LANES, drop_blocks=lead // LANES)[None]
    y_sample = rms_norm_rows(h2_s, final_norm_g, F32, tm_s).reshape(n_dec, ROWS_S, D_MODEL)[:, :dec_seq]

    heads = lambda a: a.reshape(a.shape[:-1] + (H_B, HD_B))
    p_k = heads(zb_p[pad_p:, ZB_K * D_B:(ZB_K + 1) * D_B])[None]
    p_v = heads(zb_p[pad_p:, ZB_V * D_B:(ZB_V + 1) * D_B])[None]
    p_logf = logf_p[pad_p:, :H_B][None]
    zs3 = zb_s.reshape(n_dec, ROWS_S, ZB_W)[:, :dec_seq]
    s_k = heads(zs3[:, :, ZB_K * D_B:(ZB_K + 1) * D_B])
    s_v = heads(zs3[:, :, ZB_V * D_B:(ZB_V + 1) * D_B])
    s_logf = logf_s.reshape(n_dec, ROWS_S, LANES)[:, :dec_seq, :H_B]
    return (y_prompt, y_sample, p_wkv, p_shift, p_k, p_v, p_logf, p_ret,
            s_wkv, s_shift, s_k, s_v, s_logf, s_ret)
```
